```python
import math
import jax, jax.numpy as jnp
from jax import lax
import numpy as np

D_MODEL = 1024
BATCH = 16
SEQ = 256
DEPTH = 4
DEC_BATCH = 8
DEC_SEQ = 1024
PAST_LEN = 256

GRID_W = 64
A_HEADS = 8
A_QK_DIM = 64
A_V_DIM = 128
A_WIDTH = A_HEADS * A_V_DIM
ROPE_BASE = 10000.0
Q_BLOCK = 128
POOL_WINDOWS = (2, 4, 8, 16)
POOL_GROUP = 128
POOL_WIDTH = len(POOL_WINDOWS) * POOL_GROUP
CONV_WIDTH = 512
CONV_K = 31
N_BRANCH = 3
N_EXPERTS = 16
EXPERT_FF = 1024
CAPACITY_FACTOR = 2
Q_COLS = A_HEADS * 2 * A_QK_DIM
K_COLS = A_HEADS * 2 * A_QK_DIM
V_COLS = A_WIDTH
CONV_IN = 2 * CONV_WIDTH
IN_WIDTH = Q_COLS + K_COLS + V_COLS + POOL_WIDTH + CONV_IN
SPLIT_POINTS = (Q_COLS, Q_COLS + K_COLS, Q_COLS + K_COLS + V_COLS, Q_COLS + K_COLS + V_COLS + POOL_WIDTH)
EPS = 1e-6

kernel_name = "hybrid_diffattn_pool_conformer_ec_moe_diffusion_step"


def rms_norm(x, g):
    xf = x.astype(jnp.float32)
    y = xf * lax.rsqrt(jnp.mean(xf * xf, axis=-1, keepdims=True) + EPS)
    return y.astype(x.dtype) * g


def layer_norm(x, g, b):
    xf = x.astype(jnp.float32)
    mu = jnp.mean(xf, axis=-1, keepdims=True)
    var = jnp.mean(jnp.square(xf - mu), axis=-1, keepdims=True)
    return ((xf - mu) * lax.rsqrt(var + EPS)).astype(x.dtype) * g + b


def axial_rope(x):
    L = x.shape[1]
    rows = L // GRID_W
    row = jnp.repeat(jnp.arange(rows), GRID_W)
    col = jnp.tile(jnp.arange(GRID_W), rows)
    half = A_QK_DIM // 2
    nf = half // 2
    inv = ROPE_BASE ** (-jnp.arange(nf, dtype=jnp.float32) / nf)

    def rot(xp, pos):
        ang = pos.astype(jnp.float32)[:, None] * inv[None, :]
        cos = jnp.cos(ang)[None, :, None, None, :].astype(x.dtype)
        sin = jnp.sin(ang)[None, :, None, None, :].astype(x.dtype)
        x1, x2 = xp[..., :nf], xp[..., nf:]
        return jnp.concatenate([x1 * cos - x2 * sin, x1 * sin + x2 * cos], axis=-1)

    return jnp.concatenate([rot(x[..., :half], row), rot(x[..., half:], col)], axis=-1)


def diff_attention(q, k, v, lam):
    B, Lq, H = q.shape[:3]
    nblk = Lq // Q_BLOCK
    scale = A_QK_DIM ** -0.5
    qb = jnp.moveaxis(q.reshape(B, nblk, Q_BLOCK, H, 2, A_QK_DIM), 1, 0)

    def block(qblk):
        s = jnp.einsum('bqhmd,bkhmd->bmhqk', qblk, k).astype(jnp.float32) * scale
        p = jax.nn.softmax(s, axis=-1)
        a = p[:, 0] - lam * p[:, 1]
        return jnp.einsum('bhqk,bkhe->bqhe', a.astype(v.dtype), v)

    o = lax.map(block, qb)
    return jnp.moveaxis(o, 0, 1).reshape(B, Lq, H, A_V_DIM)


def multiscale_pool(u, pool_w, pool_scale):
    B, L, _ = u.shape
    ug = u.reshape(B, L, len(POOL_WINDOWS), POOL_GROUP)
    cs = jnp.cumsum(ug.astype(jnp.float32), axis=1)
    S = jnp.concatenate([jnp.zeros_like(cs[:, :1]), cs], axis=1)
    t = jnp.arange(L)
    outs = []
    for g, w in enumerate(POOL_WINDOWS):
        lo = jnp.clip(t - w // 2, 0, L)
        hi = jnp.clip(t + w // 2, 0, L)
        Sg = S[:, :, g]
        cnt = (hi - lo).astype(jnp.float32)[None, :, None]
        outs.append((Sg[:, hi] - Sg[:, lo]) / cnt)
    pooled = (jnp.stack(outs, axis=2) - ug.astype(jnp.float32)).astype(u.dtype)
    y = jnp.einsum('blgc,gcd->blgd', pooled, pool_w).reshape(B, L, POOL_WIDTH)
    return y * pool_scale


def conformer_conv(u, conv_w, conv_b, ln_g, ln_b):
    a, b = jnp.split(u, 2, axis=-1)
    z = a * jax.nn.sigmoid(b)
    pad = (CONV_K - 1) // 2
    z = lax.conv_general_dilated(z, conv_w[:, None, :], window_strides=(1,), padding=[(pad, pad)],
                                 dimension_numbers=('NWC', 'WIO', 'NWC'),
                                 feature_group_count=CONV_WIDTH) + conv_b
    return jax.nn.silu(layer_norm(z, ln_g, ln_b))


def expert_choice_ffn(h, w_router, w1, w3, w2):
    B, L, D = h.shape
    cap = CAPACITY_FACTOR * L // N_EXPERTS
    aff = jax.nn.softmax(jnp.einsum('bld,de->ble', h, w_router).astype(jnp.float32), axis=-1)
    gate, idx = lax.top_k(jnp.swapaxes(aff, 1, 2), cap)
    xs = jax.vmap(lambda hb, ib: hb[ib])(h, idx)
    hid = jax.nn.silu(jnp.einsum('becd,edf->becf', xs, w1)) * jnp.einsum('becd,edf->becf', xs, w3)
    ye = jnp.einsum('becf,efd->becd', hid, w2) * gate[..., None].astype(h.dtype)

    def combine(ib, yb):
        return jnp.zeros((L, D), yb.dtype).at[ib.reshape(-1)].add(yb.reshape(-1, D))

    return jax.vmap(combine)(idx, ye)


def trunk_layer(x, cvec, lp, lam_init, ctx_kv):
    B, L, _ = x.shape
    mod = jnp.einsum('bd,de->be', jax.nn.silu(cvec), lp['w_ada']) + lp['b_ada']
    sh1, sc1, g1, sh2, sc2, g2 = [m[:, None, :] for m in jnp.split(mod, 6, axis=-1)]

    h = rms_norm(x, lp['rms1_g']) * (1 + sc1) + sh1
    u = h @ lp['w_in']
    q, k, v, up, uc = jnp.split(u, SPLIT_POINTS, axis=-1)
    q = rms_norm(q.reshape(B, L, A_HEADS, 2, A_QK_DIM), lp['q_norm_g'])
    k = rms_norm(k.reshape(B, L, A_HEADS, 2, A_QK_DIM), lp['k_norm_g'])
    v = v.reshape(B, L, A_HEADS, A_V_DIM)
    own_kv = (k, v)
    if ctx_kv is None:
        k_all, v_all = k, v
    else:
        q = axial_rope(q)
        k_all = jnp.concatenate([ctx_kv[0], axial_rope(k)], axis=1)
        v_all = jnp.concatenate([ctx_kv[1], v], axis=1)
    lv = lp['lam_vec'].astype(jnp.float32)
    lam = jnp.exp(jnp.sum(lv[0] * lv[1])) - jnp.exp(jnp.sum(lv[2] * lv[3])) + lam_init
    o = diff_attention(q, k_all, v_all, lam)
    o = rms_norm(o, lp['subln_g']) * (1.0 - lam_init)
    y_a = o.reshape(B, L, A_WIDTH) @ lp['w_proj_a']
    y_b = multiscale_pool(up, lp['pool_w'], lp['pool_scale']) @ lp['w_proj_b']
    y_c = conformer_conv(uc, lp['conv_w'], lp['conv_b'], lp['conv_ln_g'], lp['conv_ln_b']) @ lp['w_proj_c']
    ga, gb, gc = jnp.split(jax.nn.sigmoid(h @ lp['w_gate'] + lp['b_gate']), N_BRANCH, axis=-1)
    x = x + g1 * ((ga * y_a + gb * y_b + gc * y_c) @ lp['w_out'])

    h2 = rms_norm(x, lp['rms2_g']) * (1 + sc2) + sh2
    x = x + g2 * expert_choice_ffn(h2, lp['w_router'], lp['w1'], lp['w3'], lp['w2'])
    return x, own_kv


def setup_inputs(seed: int = 0) -> dict:
    key = jax.random.key(seed)
    ks = jax.random.split(key, 32)
    f32 = jnp.float32

    def nrm(k, shape, scale):
        return jax.random.normal(k, shape, f32) * scale

    D = D_MODEL
    return {
        'x_prompt': nrm(ks[0], (BATCH, SEQ, D), 1.0),
        'x_sample': nrm(ks[1], (DEC_BATCH, DEC_SEQ, D), 1.0),
        'c': nrm(ks[2], (DEC_BATCH, D), 1.0),
        'cache_k': nrm(ks[3], (DEC_BATCH, DEPTH, PAST_LEN, A_HEADS, 2, A_QK_DIM), 1.0),
        'cache_v': nrm(ks[4], (DEC_BATCH, DEPTH, PAST_LEN, A_HEADS, A_V_DIM), 1.0),
        'c_ctx': nrm(ks[5], (D,), 1.0),
        'rms1_g': 1.0 + nrm(ks[6], (DEPTH, D), 0.01),
        'rms2_g': 1.0 + nrm(ks[7], (DEPTH, D), 0.01),
        'w_ada': nrm(ks[8], (DEPTH, D, 6 * D), 0.5 * D ** -0.5),
        'b_ada': nrm(ks[9], (DEPTH, 6 * D), 0.01),
        'w_in': nrm(ks[10], (DEPTH, D, IN_WIDTH), D ** -0.5),
        'q_norm_g': 1.0 + nrm(ks[11], (DEPTH, A_QK_DIM), 0.01),
        'k_norm_g': 1.0 + nrm(ks[12], (DEPTH, A_QK_DIM), 0.01),
        'lam_vec': nrm(ks[13], (DEPTH, 4, A_QK_DIM), 0.1),
        'subln_g': 1.0 + nrm(ks[14], (DEPTH, A_V_DIM), 0.01),
        'w_proj_a': nrm(ks[15], (DEPTH, A_WIDTH, D), A_WIDTH ** -0.5),
        'pool_w': nrm(ks[16], (DEPTH, len(POOL_WINDOWS), POOL_GROUP, POOL_GROUP), POOL_GROUP ** -0.5),
        'pool_scale': 1.0 + nrm(ks[17], (DEPTH, POOL_WIDTH), 0.01),
        'w_proj_b': nrm(ks[18], (DEPTH, POOL_WIDTH, D), POOL_WIDTH ** -0.5),
        'conv_w': nrm(ks[19], (DEPTH, CONV_K, CONV_WIDTH), CONV_K ** -0.5),
        'conv_b': nrm(ks[20], (DEPTH, CONV_WIDTH), 0.01),
        'conv_ln_g': 1.0 + nrm(ks[21], (DEPTH, CONV_WIDTH), 0.01),
        'conv_ln_b': nrm(ks[22], (DEPTH, CONV_WIDTH), 0.01),
        'w_proj_c': nrm(ks[23], (DEPTH, CONV_WIDTH, D), CONV_WIDTH ** -0.5),
        'w_gate': nrm(ks[24], (DEPTH, D, N_BRANCH * D), D ** -0.5),
        'b_gate': nrm(ks[25], (DEPTH, N_BRANCH * D), 0.01),
        'w_out': nrm(ks[26], (DEPTH, D, D), D ** -0.5),
        'w_router': nrm(ks[27], (DEPTH, D, N_EXPERTS), D ** -0.5),
        'w1': nrm(ks[28], (DEPTH, N_EXPERTS, D, EXPERT_FF), D ** -0.5),
        'w3': nrm(ks[29], (DEPTH, N_EXPERTS, D, EXPERT_FF), D ** -0.5),
        'w2': nrm(ks[30], (DEPTH, N_EXPERTS, EXPERT_FF, D), EXPERT_FF ** -0.5),
    }


def reference(x_prompt, x_sample, c, cache_k, cache_v, c_ctx,
              rms1_g, rms2_g, w_ada, b_ada, w_in, q_norm_g, k_norm_g, lam_vec, subln_g, w_proj_a,
              pool_w, pool_scale, w_proj_b, conv_w, conv_b, conv_ln_g, conv_ln_b, w_proj_c,
              w_gate, b_gate, w_out, w_router, w1, w3, w2):
    params = {
        'rms1_g': rms1_g, 'rms2_g': rms2_g, 'w_ada': w_ada, 'b_ada': b_ada, 'w_in': w_in,
        'q_norm_g': q_norm_g, 'k_norm_g': k_norm_g, 'lam_vec': lam_vec, 'subln_g': subln_g,
        'w_proj_a': w_proj_a, 'pool_w': pool_w, 'pool_scale': pool_scale, 'w_proj_b': w_proj_b,
        'conv_w': conv_w, 'conv_b': conv_b, 'conv_ln_g': conv_ln_g, 'conv_ln_b': conv_ln_b,
        'w_proj_c': w_proj_c, 'w_gate': w_gate, 'b_gate': b_gate, 'w_out': w_out,
        'w_router': w_router, 'w1': w1, 'w3': w3, 'w2': w2,
    }
    x_ctx = x_prompt
    x_lat = x_sample
    cvec_ctx = c_ctx[None, :]
    ks, vs = [], []
    for l in range(DEPTH):
        lp = {name: arr[l] for name, arr in params.items()}
        lam_init = 0.8 - 0.6 * math.exp(-0.3 * l)
        x_ctx, (k_l, v_l) = trunk_layer(x_ctx, cvec_ctx, lp, lam_init, None)
        ks.append(k_l)
        vs.append(v_l)
        x_lat, _ = trunk_layer(x_lat, c, lp, lam_init, (cache_k[:, l], cache_v[:, l]))
    new_k = jnp.stack(ks, axis=1)
    new_v = jnp.stack(vs, axis=1)
    return (x_ctx, x_lat, new_k, new_v)
```

```python
import functools
import math

import jax
import jax.numpy as jnp
from jax import lax
from jax.experimental import pallas as pl
from jax.experimental.pallas import tpu as pltpu

F32 = jnp.float32
BF16 = jnp.bfloat16

D_MODEL = 1024
DEPTH = 4
GRID_W = 64
A_HEADS = 8
A_QK_DIM = 64
A_V_DIM = 128
ROPE_BASE = 10000.0
POOL_WINDOWS = (2, 4, 8, 16)
POOL_GROUP = 128
POOL_WIDTH = len(POOL_WINDOWS) * POOL_GROUP
CONV_WIDTH = 512
CONV_K = 31
N_EXPERTS = 16
EXPERT_FF = 1024
CAPACITY_FACTOR = 2
IN_WIDTH = 3 * D_MODEL + POOL_WIDTH + 2 * CONV_WIDTH
EPS = 1e-6

LANES = 128
SUBLANES = 8
ROW_TILE = 256
SUB_ROWS = 64
POOL_HALO = 8
CONV_HALO = 16
FF_SPLIT = 2
VMEM_LIMIT = 48 * 1024 * 1024


def _cparams(*sem):
    return pltpu.CompilerParams(dimension_semantics=sem, vmem_limit_bytes=VMEM_LIMIT)


def _const_spec(shape):
    n = len(shape)
    return pl.BlockSpec(shape, lambda *_: (0,) * n, pipeline_mode=pl.Buffered(1))


def _layer_spec(shape, l):
    n = len(shape)
    return pl.BlockSpec((None,) + tuple(shape), lambda *_: (l,) + (0,) * n,
                        pipeline_mode=pl.Buffered(1))


def _dot(a, b):
    return jnp.dot(a, b, preferred_element_type=F32)


def _dot_nt(a, b):
    return lax.dot_general(a, b, (((1,), (1,)), ((), ())), preferred_element_type=F32)


def _split_bf16(x):
    hi = x.astype(BF16)
    lo = (x - hi.astype(F32)).astype(BF16)
    return hi, lo


def _mod_kernel(cv_ref, w_ref, b_ref, out_ref):
    cv = cv_ref[...]
    s_hi, s_lo = _split_bf16(cv * jax.nn.sigmoid(cv))
    w_hi, w_lo = _split_bf16(w_ref[...])
    out_ref[...] = _dot(s_hi, w_hi) + _dot(s_lo, w_hi) + _dot(s_hi, w_lo) + b_ref[...]


def _mod_call(cv, w_ada, b_ada):
    rows = cv.shape[0]
    width = w_ada.shape[-1]
    tn = 1536
    return pl.pallas_call(
        _mod_kernel,
        grid=(DEPTH, width // tn),
        in_specs=[
            pl.BlockSpec((rows, D_MODEL), lambda l, n: (0, 0)),
            pl.BlockSpec((None, D_MODEL, tn), lambda l, n: (l, 0, n)),
            pl.BlockSpec((None, 1, tn), lambda l, n: (l, 0, n)),
        ],
        out_specs=pl.BlockSpec((None, rows, tn), lambda l, n: (l, 0, n)),
        out_shape=jax.ShapeDtypeStruct((DEPTH, rows, width), F32),
        compiler_params=_cparams("arbitrary", "arbitrary"),
        name="adaln_mod",
    )(cv, w_ada, b_ada.reshape(DEPTH, 1, width))


def _adaln(x, gain, shift, scale):
    ms = jnp.mean(x * x, axis=-1, keepdims=True)
    return (x * lax.rsqrt(ms + EPS)) * gain * (1.0 + scale) + shift


def _in_proj_kernel(latent, *refs):
    if latent:
        (x_ref, mod_ref, g1_ref, win_ref, qg_ref, kg_ref, cos_ref, sa_ref, sb_ref,
         q_ref, k_ref, v_ref, up_ref, z_ref) = refs
    else:
        (x_ref, mod_ref, g1_ref, win_ref, qg_ref, kg_ref,
         q_ref, k_ref, v_ref, up_ref, z_ref, nk_ref, nv_ref) = refs
    D = D_MODEL
    h = _adaln(x_ref[...], g1_ref[...], mod_ref[:, 0:D], mod_ref[:, D:2 * D])
    u = _dot(h.astype(BF16), win_ref[...])

    low_half = lax.broadcasted_iota(jnp.int32, (1, LANES), 1) < A_QK_DIM

    def qk_norm(s, gain):
        t = s * s
        t_lo = jnp.sum(jnp.where(low_half, t, 0.0), axis=-1, keepdims=True)
        t_hi = jnp.sum(jnp.where(low_half, 0.0, t), axis=-1, keepdims=True)
        ms = jnp.where(low_half, t_lo, t_hi) * (1.0 / A_QK_DIM)
        return (s * lax.rsqrt(ms + EPS)) * gain

    def rope(y):
        return (y * cos_ref[...] + pltpu.roll(y, LANES - 16, 1) * sa_ref[...]
                + pltpu.roll(y, 16, 1) * sb_ref[...])

    scale = A_QK_DIM ** -0.5
    for hh in range(A_HEADS):
        sl = slice(hh * LANES, (hh + 1) * LANES)
        qs = qk_norm(u[:, hh * LANES:(hh + 1) * LANES], qg_ref[...])
        ks = qk_norm(u[:, D + hh * LANES:D + (hh + 1) * LANES], kg_ref[...])
        if latent:
            qs = rope(qs)
            ks = rope(ks)
        else:
            nk_ref[:, sl] = ks
        q_ref[:, sl] = (qs * scale).astype(BF16)
        k_ref[:, sl] = ks.astype(BF16)
    v = u[:, 2 * D:3 * D]
    v_ref[...] = v.astype(BF16)
    if not latent:
        nv_ref[...] = v
    up_ref[...] = u[:, 3 * D:3 * D + POOL_WIDTH]
    c0 = 3 * D + POOL_WIDTH
    z_ref[...] = u[:, c0:c0 + CONV_WIDTH] * jax.nn.sigmoid(u[:, c0 + CONV_WIDTH:c0 + 2 * CONV_WIDTH])


def _in_proj_call(latent, l, x, mod, mod_row, rms1_g, w_in_b, qg2, kg2, rope_tabs):
    B, L, D = x.shape
    tm = ROW_TILE
    row = lambda b, j: (b, j, 0)
    in_specs = [
        pl.BlockSpec((None, tm, D), row),
        pl.BlockSpec((None, None, 1, 6 * D), lambda b, j: (l, mod_row(b), 0, 0)),
        _layer_spec((1, D), l),
        _layer_spec((D, IN_WIDTH), l),
        _layer_spec((1, LANES), l),
        _layer_spec((1, LANES), l),
    ]
    args = [x, mod, rms1_g, w_in_b, qg2, kg2]
    if latent:
        in_specs += [pl.BlockSpec((tm, LANES), lambda b, j: (j, 0))] * 3
        args += list(rope_tabs)
    out_specs = [pl.BlockSpec((None, tm, D), row)] * 3 + [pl.BlockSpec((None, tm, POOL_WIDTH), row),
                                                         pl.BlockSpec((None, tm, CONV_WIDTH), row)]
    out_shape = [jax.ShapeDtypeStruct((B, L, D), BF16)] * 3 + [
        jax.ShapeDtypeStruct((B, L, POOL_WIDTH), F32), jax.ShapeDtypeStruct((B, L, CONV_WIDTH), F32)]
    if not latent:
        out_specs += [pl.BlockSpec((None, tm, D), row)] * 2
        out_shape += [jax.ShapeDtypeStruct((B, L, D), F32)] * 2
    return pl.pallas_call(
        functools.partial(_in_proj_kernel, latent),
        grid=(B, L // tm),
        in_specs=in_specs, out_specs=out_specs, out_shape=out_shape,
        compiler_params=_cparams("parallel", "parallel"),
        name="in_proj_lat" if latent else "in_proj_ctx",
    )(*args)


def _attn_kernel(latent, lam_init, *refs):
    if latent:
        q_ref, k_ref, v_ref, ck_ref, cv_ref, lv_ref, sg_ref, o_ref = refs
    else:
        q_ref, k_ref, v_ref, lv_ref, sg_ref, o_ref = refs
    lv = lv_ref[...]
    lam = (jnp.exp(jnp.sum(lv[0:1] * lv[1:2], axis=-1, keepdims=True))
           - jnp.exp(jnp.sum(lv[2:3] * lv[3:4], axis=-1, keepdims=True)) + lam_init)
    low_half = lax.broadcasted_iota(jnp.int32, (1, LANES), 1) < A_QK_DIM

    def head_slice(hh):
        return slice(hh * LANES, (hh + 1) * LANES)

    def scores(hh, half):
        sl = head_slice(hh)
        qh = q_ref[:, sl]
        zero = jnp.zeros_like(qh)
        qm = jnp.where(low_half, qh, zero) if half == 0 else jnp.where(low_half, zero, qh)
        keys = [k_ref[:, sl]]
        if latent:
            keys.insert(0, ck_ref[:, sl].astype(BF16))
        return [_dot_nt(qm, kp) for kp in keys]

    def attend(hh, s):
        sl = head_slice(hh)
        vals = [v_ref[:, sl]]
        if latent:
            vals.insert(0, cv_ref[:, sl].astype(BF16))
        mx = functools.reduce(jnp.maximum, [jnp.max(sp, axis=-1, keepdims=True) for sp in s])
        if not latent:
            e = [jnp.exp(sp - mx) for sp in s]
            den = functools.reduce(jnp.add, [jnp.sum(ep, axis=-1, keepdims=True) for ep in e])
            pv = functools.reduce(jnp.add, [_dot(ep.astype(BF16), vp) for ep, vp in zip(e, vals)])
            return pv / den
        e = [jnp.exp((sp - mx).astype(BF16)) for sp in s]
        pv = functools.reduce(jnp.add, [
            _dot(ep, jnp.concatenate([vp, jnp.ones_like(vp)], axis=1)) for ep, vp in zip(e, vals)])
        return pv[:, 0:A_V_DIM] / pv[:, A_V_DIM:A_V_DIM + 1]

    units = [(hh, half) for hh in range(A_HEADS) for half in range(2)]
    s_next = scores(*units[0])
    first = None
    for i, (hh, half) in enumerate(units):
        s = s_next
        if i + 1 < len(units):
            s_next = scores(*units[i + 1])
        out = attend(hh, s)
        if half == 0:
            first = out
            continue
        o = first - lam * out
        ms = jnp.mean(o * o, axis=-1, keepdims=True)
        o = (o * lax.rsqrt(ms + EPS)) * sg_ref[...] * (1.0 - lam_init)
        o_ref[:, head_slice(hh)] = o.astype(BF16)


def _attn_call(latent, l, lam_init, q, k, v, cache_k, cache_v, lam_vec, subln_g):
    B, L, D = q.shape
    tq = ROW_TILE
    full = pl.BlockSpec((None, L, D), lambda b, j: (b, 0, 0))
    in_specs = [pl.BlockSpec((None, tq, D), lambda b, j: (b, j, 0)), full, full]
    args = [q, k, v]
    if latent:
        P = cache_k.shape[2]
        cspec = pl.BlockSpec((None, None, P, D), lambda b, j: (b, l, 0, 0))
        in_specs += [cspec, cspec]
        args += [cache_k, cache_v]
    in_specs += [_layer_spec((4, A_QK_DIM), l), _layer_spec((1, A_V_DIM), l)]
    args += [lam_vec, subln_g]
    return pl.pallas_call(
        functools.partial(_attn_kernel, latent, lam_init),
        grid=(B, L // tq),
        in_specs=in_specs,
        out_specs=pl.BlockSpec((None, tq, D), lambda b, j: (b, j, 0)),
        out_shape=jax.ShapeDtypeStruct((B, L, D), BF16),
        compiler_params=_cparams("parallel", "parallel"),
        name="attn_lat" if latent else "attn_ctx",
    )(*args)


def _mix_kernel(L, x_ref, o_ref, up_ref, z_ref, mod_ref, g1_ref, wg_ref, bg_ref, wa_ref,
                pw_ref, ps_ref, wb_ref, cw_ref, cb_ref, lg_ref, lb_ref, wc_ref, wo_ref,
                g2_ref, wr_ref, xo_ref, h2_ref, aff_ref, pool_s, conv_s, shift_s, acc_s):
    D = D_MODEL
    tc = ROW_TILE
    j = pl.program_id(1)
    last = pl.num_programs(1) - 1
    r0 = pl.multiple_of(j * tc, tc)

    x = x_ref[...]
    h = _adaln(x, g1_ref[...], mod_ref[:, 0:D], mod_ref[:, D:2 * D])
    gates = jax.nn.sigmoid(_dot(h.astype(BF16), wg_ref[...]) + bg_ref[...])
    y_a = _dot(o_ref[...], wa_ref[...])

    def window_rows(src_ref, dst_ref, halo):
        top = src_ref[pl.ds(pl.multiple_of(jnp.maximum(r0 - halo, 0), halo), halo), :]
        bot = src_ref[pl.ds(pl.multiple_of(jnp.minimum(r0 + tc, L - halo), halo), halo), :]
        dst_ref[0:halo, :] = jnp.where(j > 0, top, 0.0)
        dst_ref[halo:halo + tc, :] = src_ref[pl.ds(r0, tc), :]
        dst_ref[halo + tc:2 * halo + tc, :] = jnp.where(j < last, bot, 0.0)

    window_rows(up_ref, pool_s, POOL_HALO)
    t = r0 + lax.broadcasted_iota(jnp.int32, (SUB_ROWS, 1), 0)
    for g, w in enumerate(POOL_WINDOWS):
        cs = slice(g * POOL_GROUP, (g + 1) * POOL_GROUP)
        for rb in range(tc // SUB_ROWS):
            base = POOL_HALO + rb * SUB_ROWS
            acc = pool_s[base - w // 2:base - w // 2 + SUB_ROWS, cs]
            for off in range(1 - w // 2, w // 2):
                acc = acc + pool_s[base + off:base + off + SUB_ROWS, cs]
            tt = t + rb * SUB_ROWS
            cnt = (jnp.minimum(tt + w // 2, L) - jnp.maximum(tt - w // 2, 0)).astype(F32)
            acc_s[rb * SUB_ROWS:(rb + 1) * SUB_ROWS, cs] = (
                acc / cnt - pool_s[base:base + SUB_ROWS, cs])
    pooled = acc_s[...].astype(BF16)
    yb = jnp.concatenate(
        [_dot(pooled[:, g * POOL_GROUP:(g + 1) * POOL_GROUP], pw_ref[g]) for g in range(len(POOL_WINDOWS))],
        axis=-1) * ps_ref[...]
    y_b = _dot(yb.astype(BF16), wb_ref[...])

    window_rows(z_ref, conv_s, CONV_HALO)
    n_shift = tc + 2 * CONV_HALO - SUBLANES
    for b in range(1, SUBLANES):
        shift_s[b - 1] = conv_s[b:b + n_shift, :]
    first_off = CONV_HALO - (CONV_K - 1) // 2
    for c in range(CONV_WIDTH // LANES):
        cs = slice(c * LANES, (c + 1) * LANES)
        for rb in range(tc // SUB_ROWS):
            acc = None
            for tap in range(CONV_K):
                b = (first_off + tap) % SUBLANES
                lo = first_off + tap - b + rb * SUB_ROWS
                src = conv_s[lo:lo + SUB_ROWS, cs] if b == 0 else shift_s[b - 1, lo:lo + SUB_ROWS, cs]
                term = src * cw_ref[tap:tap + 1, cs]
                acc = term if acc is None else acc + term
            acc_s[rb * SUB_ROWS:(rb + 1) * SUB_ROWS, cs] = acc
    zc = acc_s[...] + cb_ref[...]
    mu = jnp.mean(zc, axis=-1, keepdims=True)
    var = jnp.mean(jnp.square(zc - mu), axis=-1, keepdims=True)
    zn = ((zc - mu) * lax.rsqrt(var + EPS)) * lg_ref[...] + lb_ref[...]
    y_c = _dot((zn * jax.nn.sigmoid(zn)).astype(BF16), wc_ref[...])

    merged = gates[:, 0:D] * y_a + gates[:, D:2 * D] * y_b + gates[:, 2 * D:3 * D] * y_c
    x = x + mod_ref[:, 2 * D:3 * D] * _dot(merged.astype(BF16), wo_ref[...])
    xo_ref[...] = x

    h2 = _adaln(x, g2_ref[...], mod_ref[:, 3 * D:4 * D], mod_ref[:, 4 * D:5 * D])
    h2_hi, h2_lo = _split_bf16(h2)
    h2_ref[...] = h2_hi
    wr_hi, wr_lo = _split_bf16(wr_ref[...])
    logits = _dot_nt(wr_hi, h2_hi) + _dot_nt(wr_hi, h2_lo) + _dot_nt(wr_lo, h2_hi)
    e = jnp.exp(logits - jnp.max(logits, axis=0, keepdims=True))
    aff_ref[...] = e / jnp.sum(e, axis=0, keepdims=True)


def _mix_call(name, l, x, o, up, z, mod, mod_row, p):
    B, L, D = x.shape
    tc = ROW_TILE
    row = lambda b, j: (b, j, 0)
    req = lambda b, j: (b, 0, 0)
    in_specs = [
        pl.BlockSpec((None, tc, D), row),
        pl.BlockSpec((None, tc, D), row),
        pl.BlockSpec((None, L, POOL_WIDTH), req),
        pl.BlockSpec((None, L, CONV_WIDTH), req),
        pl.BlockSpec((None, None, 1, 6 * D), lambda b, j: (l, mod_row(b), 0, 0)),
        _layer_spec((1, D), l),
        _layer_spec((D, 3 * D), l),
        _layer_spec((1, 3 * D), l),
        _layer_spec((D, D), l),
        _layer_spec((len(POOL_WINDOWS), POOL_GROUP, POOL_GROUP), l),
        _layer_spec((1, POOL_WIDTH), l),
        _layer_spec((POOL_WIDTH, D), l),
        _layer_spec((CONV_K, CONV_WIDTH), l),
        _layer_spec((1, CONV_WIDTH), l),
        _layer_spec((1, CONV_WIDTH), l),
        _layer_spec((1, CONV_WIDTH), l),
        _layer_spec((CONV_WIDTH, D), l),
        _layer_spec((D, D), l),
        _layer_spec((1, D), l),
        _layer_spec((N_EXPERTS, D), l),
    ]
    args = [x, o, up, z, mod, p['rms1_g'], p['w_gate'], p['b_gate'], p['w_proj_a'], p['pool_w'],
            p['pool_scale'], p['w_proj_b'], p['conv_w'], p['conv_b'], p['conv_ln_g'], p['conv_ln_b'],
            p['w_proj_c'], p['w_out'], p['rms2_g'], p['w_router_t']]
    return pl.pallas_call(
        functools.partial(_mix_kernel, L),
        grid=(B, L // tc),
        in_specs=in_specs,
        out_specs=[pl.BlockSpec((None, tc, D), row), pl.BlockSpec((None, tc, D), row),
                   pl.BlockSpec((None, N_EXPERTS, tc), lambda b, j: (b, 0, j))],
        out_shape=[jax.ShapeDtypeStruct((B, L, D), F32), jax.ShapeDtypeStruct((B, L, D), BF16),
                   jax.ShapeDtypeStruct((B, N_EXPERTS, L), F32)],
        scratch_shapes=[pltpu.VMEM((tc + 2 * POOL_HALO, POOL_WIDTH), F32),
                        pltpu.VMEM((tc + 2 * CONV_HALO, CONV_WIDTH), F32),
                        pltpu.VMEM((SUBLANES - 1, tc + 2 * CONV_HALO - SUBLANES, CONV_WIDTH), F32),
                        pltpu.VMEM((tc, CONV_WIDTH), F32)],
        compiler_params=_cparams("parallel", "arbitrary"),
        name=name,
    )(*args)


def _threshold_kernel(cap, aff_ref, thr_ref):
    aff = aff_ref[...]

    def refine(i, bits):
        cand = bits | jnp.left_shift(jnp.int32(1), 30 - i)
        cnt = jnp.sum(jnp.where(aff >= pltpu.bitcast(cand, F32), 1.0, 0.0), axis=1, keepdims=True)
        return jnp.where(cnt >= cap, cand, bits)

    bits = lax.fori_loop(0, 31, refine, jnp.zeros((aff.shape[0], 1), jnp.int32))
    thr_ref[...] = pltpu.bitcast(bits, F32)


def _threshold_call(name, aff):
    B, E, L = aff.shape
    cap = CAPACITY_FACTOR * L // N_EXPERTS
    thr = pl.pallas_call(
        functools.partial(_threshold_kernel, cap),
        out_shape=jax.ShapeDtypeStruct((B * E, 1), F32),
        compiler_params=pltpu.CompilerParams(vmem_limit_bytes=VMEM_LIMIT),
        name=name,
    )(aff.reshape(B * E, L))
    return thr.reshape(B, E, 1)


def _route_kernel(cap, aff_ref, thr_ref, h2_ref, tri_ref, xs_ref, p_ref, gate_ref):
    aff = aff_ref[...]
    thr = thr_ref[...]
    above = aff > thr
    tied = aff == thr
    n_above = jnp.sum(jnp.where(above, 1.0, 0.0), axis=1, keepdims=True)
    tri = tri_ref[...]
    tied_rank = _dot(jnp.where(tied, 1.0, 0.0).astype(BF16), tri)
    sel = above | (tied & (tied_rank <= cap - n_above))
    slot = _dot(jnp.where(sel, 1.0, 0.0).astype(BF16), tri) - 1.0
    c_iota = lax.broadcasted_iota(jnp.int32, (cap, 1), 0).astype(F32)
    tf = ROW_TILE
    n_blk = p_ref.shape[0]
    for e in range(N_EXPERTS):
        rows = slice(e * cap, (e + 1) * cap)
        onehot = (slot[e:e + 1, :] == c_iota) & sel[e:e + 1, :]
        onehot_b = jnp.where(onehot, 1.0, 0.0).astype(BF16)
        for jb in range(n_blk):
            p_ref[jb, rows, :] = onehot_b[:, jb * tf:(jb + 1) * tf]
        gate_ref[rows, :] = jnp.sum(jnp.where(onehot, aff[e:e + 1, :], 0.0), axis=1, keepdims=True)
    xs = _dot(p_ref[0], h2_ref[0:tf, :])
    for jb in range(1, n_blk):
        xs = xs + _dot(p_ref[jb], h2_ref[jb * tf:(jb + 1) * tf, :])
    xs_ref[...] = xs.astype(BF16)


def _route_call(name, aff, thr, h2, tri):
    B, E, L = aff.shape
    D = h2.shape[-1]
    cap = CAPACITY_FACTOR * L // N_EXPERTS
    tf = ROW_TILE
    req = lambda b: (b, 0, 0)
    return pl.pallas_call(
        functools.partial(_route_kernel, cap),
        grid=(B,),
        in_specs=[pl.BlockSpec((None, E, L), req), pl.BlockSpec((None, E, 1), req),
                  pl.BlockSpec((None, L, D), req), _const_spec((L, L))],
        out_specs=[pl.BlockSpec((None, E * cap, D), req),
                   pl.BlockSpec((None, L // tf, E * cap, tf), lambda b: (b, 0, 0, 0)),
                   pl.BlockSpec((None, E * cap, 1), req)],
        out_shape=[jax.ShapeDtypeStruct((B, E * cap, D), BF16),
                   jax.ShapeDtypeStruct((B, L // tf, E * cap, tf), BF16),
                   jax.ShapeDtypeStruct((B, E * cap, 1), F32)],
        compiler_params=_cparams("parallel"),
        name=name,
    )(aff, thr, h2, tri)


def _expert_kernel(xc_ref, xl_ref, gc_ref, gl_ref, w1_ref, w3_ref, w2_ref, yc_ref, yl_ref, acc_c, acc_l):
    assert FF_SPLIT == 2
    f = pl.program_id(1)
    w1 = w1_ref[...].astype(BF16)
    w3 = w3_ref[...].astype(BF16)
    w2 = w2_ref[...].astype(BF16)
    D = D_MODEL
    nbl = xl_ref.shape[0] // 2
    chunks = [(xc_ref, gc_ref, yc_ref, acc_c, 0, xc_ref.shape[0]),
              (xl_ref, gl_ref, yl_ref, acc_l, 0, nbl), (xl_ref, gl_ref, yl_ref, acc_l, nbl, nbl)]

    def up(chunk):
        x_ref, _, _, _, b0, nb = chunk
        x = x_ref[b0:b0 + nb].reshape(nb * x_ref.shape[1], D)
        return _dot(x, w1), _dot(x, w3)

    def down(chunk, a, b):
        _, g_ref, y_ref, acc, b0, nb = chunk
        cap = g_ref.shape[1]
        rows = slice(b0 * cap, (b0 + nb) * cap)
        part = _dot(((a * jax.nn.sigmoid(a)) * b).astype(BF16), w2)

        @pl.when(f == 0)
        def _():
            acc[rows, :] = part

        @pl.when(f == FF_SPLIT - 1)
        def _():
            y = (acc[rows, :] + part).reshape(nb, cap, D) * g_ref[b0:b0 + nb]
            y_ref[b0:b0 + nb] = y.astype(BF16)

    ab = up(chunks[0])
    for i, chunk in enumerate(chunks):
        ab_next = up(chunks[i + 1]) if i + 1 < len(chunks) else None
        down(chunk, *ab)
        ab = ab_next


def _expert_call(l, xs_c, xs_l, gate_c, gate_l, w1, w3, w2):
    Bc, Rc, D = xs_c.shape
    Bl, Rl, _ = xs_l.shape
    cap_c, cap_l = Rc // N_EXPERTS, Rl // N_EXPERTS
    ff = EXPERT_FF // FF_SPLIT
    tok = lambda e, f: (0, e, 0)
    return pl.pallas_call(
        _expert_kernel,
        grid=(N_EXPERTS, FF_SPLIT),
        in_specs=[
            pl.BlockSpec((Bc, cap_c, D), tok), pl.BlockSpec((Bl, cap_l, D), tok),
            pl.BlockSpec((Bc, cap_c, 1), tok), pl.BlockSpec((Bl, cap_l, 1), tok),
            pl.BlockSpec((None, None, D, ff), lambda e, f: (l, e, 0, f)),
            pl.BlockSpec((None, None, D, ff), lambda e, f: (l, e, 0, f)),
            pl.BlockSpec((None, None, ff, D), lambda e, f: (l, e, f, 0)),
        ],
        out_specs=[pl.BlockSpec((Bc, cap_c, D), tok), pl.BlockSpec((Bl, cap_l, D), tok)],
        out_shape=[jax.ShapeDtypeStruct(xs_c.shape, BF16), jax.ShapeDtypeStruct(xs_l.shape, BF16)],
        scratch_shapes=[pltpu.VMEM((Bc * cap_c, D), F32), pltpu.VMEM((Bl * cap_l, D), F32)],
        compiler_params=_cparams("parallel", "arbitrary"),
        name="experts",
    )(xs_c, xs_l, gate_c, gate_l, w1, w3, w2)


def _combine_kernel(x_ref, p_ref, ye_ref, mod_ref, out_ref):
    D = D_MODEL
    y = lax.dot_general(p_ref[...], ye_ref[...], (((0,), (0,)), ((), ())), preferred_element_type=F32)
    out_ref[...] = x_ref[...] + mod_ref[:, 5 * D:6 * D] * y


def _combine_call(name, l, x, p_onehot, ye, mod, mod_row):
    B, L, D = x.shape
    R = ye.shape[1]
    tf = ROW_TILE
    return pl.pallas_call(
        _combine_kernel,
        grid=(B, L // tf),
        in_specs=[pl.BlockSpec((None, tf, D), lambda b, j: (b, j, 0)),
                  pl.BlockSpec((None, None, R, tf), lambda b, j: (b, j, 0, 0)),
                  pl.BlockSpec((None, R, D), lambda b, j: (b, 0, 0)),
                  pl.BlockSpec((None, None, 1, 6 * D), lambda b, j: (l, mod_row(b), 0, 0))],
        out_specs=pl.BlockSpec((None, tf, D), lambda b, j: (b, j, 0)),
        out_shape=jax.ShapeDtypeStruct((B, L, D), F32),
        compiler_params=_cparams("parallel", "parallel"),
        name=name,
    )(x, p_onehot, ye, mod)


def _rope_tables(L):
    half = A_QK_DIM // 2
    nf = half // 2
    lane = jnp.arange(LANES)
    inv = ROPE_BASE ** (-jnp.arange(nf, dtype=F32) / nf)
    pos = jnp.arange(L)
    row = (pos // GRID_W).astype(F32)
    col = (pos % GRID_W).astype(F32)
    use_row = (lane % A_QK_DIM) < half
    p = jnp.where(use_row[None, :], row[:, None], col[:, None])
    ang = p * inv[lane % nf][None, :]
    first = ((lane % half) < nf)[None, :]
    sin = jnp.sin(ang)
    return jnp.cos(ang), jnp.where(first, -sin, 0.0), jnp.where(first, 0.0, sin)


def kernel(x_prompt, x_sample, c, cache_k, cache_v, c_ctx, rms1_g, rms2_g, w_ada, b_ada, w_in, q_norm_g,
           k_norm_g, lam_vec, subln_g, w_proj_a, pool_w, pool_scale, w_proj_b, conv_w, conv_b, conv_ln_g,
           conv_ln_b, w_proj_c, w_gate, b_gate, w_out, w_router, w1, w3, w2):
    D = D_MODEL
    Bc, Lc, _ = x_prompt.shape
    Bl, Ll, _ = x_sample.shape
    P = cache_k.shape[2]

    row3 = lambda a: a.reshape(DEPTH, 1, a.shape[-1])
    p = {
        'rms1_g': row3(rms1_g), 'rms2_g': row3(rms2_g),
        'w_gate': w_gate.astype(BF16), 'b_gate': row3(b_gate),
        'w_proj_a': w_proj_a.astype(BF16), 'pool_w': pool_w.astype(BF16), 'pool_scale': row3(pool_scale),
        'w_proj_b': w_proj_b.astype(BF16), 'conv_w': conv_w, 'conv_b': row3(conv_b),
        'conv_ln_g': row3(conv_ln_g), 'conv_ln_b': row3(conv_ln_b), 'w_proj_c': w_proj_c.astype(BF16),
        'w_out': w_out.astype(BF16), 'w_router_t': jnp.swapaxes(w_router, 1, 2),
    }
    w_in_b = w_in.astype(BF16)
    qg2 = row3(jnp.concatenate([q_norm_g, q_norm_g], axis=-1))
    kg2 = row3(jnp.concatenate([k_norm_g, k_norm_g], axis=-1))
    subln = row3(subln_g)
    rope_tabs = _rope_tables(Ll)
    tri_c = (jnp.arange(Lc)[:, None] <= jnp.arange(Lc)[None, :]).astype(BF16)
    tri_l = (jnp.arange(Ll)[:, None] <= jnp.arange(Ll)[None, :]).astype(BF16)
    ck = cache_k.reshape(Bl, DEPTH, P, D)
    cv = cache_v.reshape(Bl, DEPTH, P, D)

    n_mod = 16
    cvec = jnp.concatenate([c_ctx[None, :], c, jnp.zeros((n_mod - 1 - Bl, D), F32)], axis=0)
    mod = _mod_call(cvec, w_ada, b_ada).reshape(DEPTH, n_mod, 1, 6 * D)
    ctx_row = lambda b: 0
    lat_row = lambda b: b + 1

    x_ctx, x_lat = x_prompt, x_sample
    new_k, new_v = [], []
    for l in range(DEPTH):
        lam_init = 0.8 - 0.6 * math.exp(-0.3 * l)
        qc, kc, vc, upc, zc, nk, nv = _in_proj_call(False, l, x_ctx, mod, ctx_row, p['rms1_g'], w_in_b,
                                                    qg2, kg2, None)
        ql, kl, vl, upl, zl = _in_proj_call(True, l, x_lat, mod, lat_row, p['rms1_g'], w_in_b,
                                            qg2, kg2, rope_tabs)
        new_k.append(nk)
        new_v.append(nv)
        oc = _attn_call(False, l, lam_init, qc, kc, vc, None, None, lam_vec, subln)
        ol = _attn_call(True, l, lam_init, ql, kl, vl, ck, cv, lam_vec, subln)
        x_ctx, h2c, affc = _mix_call("mix_ctx", l, x_ctx, oc, upc, zc, mod, ctx_row, p)
        x_lat, h2l, affl = _mix_call("mix_lat", l, x_lat, ol, upl, zl, mod, lat_row, p)
        xsc, pc, gc = _route_call("route_ctx", affc, _threshold_call("threshold_ctx", affc), h2c, tri_c)
        xsl, pl_, gl = _route_call("route_lat", affl, _threshold_call("threshold_lat", affl), h2l, tri_l)
        yec, yel = _expert_call(l, xsc, xsl, gc, gl, w1, w3, w2)
        x_ctx = _combine_call("combine_ctx", l, x_ctx, pc, yec, mod, ctx_row)
        x_lat = _combine_call("combine_lat", l, x_lat, pl_, yel, mod, lat_row)

    nk = jnp.stack(new_k, axis=1).reshape(Bc, DEPTH, Lc, A_HEADS, 2, A_QK_DIM)
    nv = jnp.stack(new_v, axis=1).reshape(Bc, DEPTH, Lc, A_HEADS, A_V_DIM)
    return (x_ctx, x_lat, nk, nv)
```
